```python
import math
import jax
import jax.numpy as jnp
from jax import lax
import numpy as np

D_MODEL = 1024
BATCH = 2
SEQ = 8192
DEPTH = 4
DEC_BATCH = 128
DEC_SEQ = 4
PAST_LEN = 2048
PAGE_SIZE = 128

RET_V_W = D_MODEL // 2
RET_HEADS = 4
RET_DV = RET_V_W // RET_HEADS
RET_DK = RET_DV // 2
RET_QK_W = RET_HEADS * RET_DK
RET_CHUNK = 128
ROPE_BASE = 10000.0
DIL_W = D_MODEL - RET_V_W
DIL_DH = 64
DIL_HEADS = DIL_W // DIL_DH
DIL_PATTERNS = ((128, 1), (512, 4), (2048, 16))
WINDOW_MAX = max(w for w, _ in DIL_PATTERNS)
DIL_BLOCK = 128
DIL_SCALE = DIL_DH ** -0.5
MIX_W = RET_V_W + DIL_W
IN_W = 2 * RET_QK_W + 2 * RET_V_W + 3 * DIL_W
D_FF = ((-(-8 * D_MODEL // 3) + 255) // 256) * 256
DEEPNORM_ALPHA = (2 * DEPTH) ** 0.25
DEEPNORM_BETA = (8 * DEPTH) ** -0.25
LN_EPS = 1e-5

kernel_name = "retention_dilated_window_hybrid_step"


def layer_norm(x, g, b):
    xf = x.astype(jnp.float32)
    mu = jnp.mean(xf, axis=-1, keepdims=True)
    var = jnp.mean(jnp.square(xf - mu), axis=-1, keepdims=True)
    return ((xf - mu) * lax.rsqrt(var + LN_EPS)).astype(x.dtype) * g + b


def rotary(x, pos):
    half = x.shape[-1] // 2
    inv = ROPE_BASE ** (-jnp.linspace(0.0, 1.0, half, dtype=jnp.float32))
    ang = pos.astype(jnp.float32)[:, None] * inv[None, :]
    cos = jnp.cos(ang)[None, :, None, :]
    sin = jnp.sin(ang)[None, :, None, :]
    x1, x2 = x[..., :half], x[..., half:]
    return jnp.concatenate([x1 * cos - x2 * sin, x1 * sin + x2 * cos], axis=-1)


def retention_scan(q, k, v, s0, chunk):
    b, l, h, _ = q.shape
    n = l // chunk
    lg = jnp.log1p(-jnp.exp2(-5.0 - jnp.arange(h, dtype=jnp.float32)))
    idx = jnp.arange(chunk, dtype=jnp.float32)
    rel = idx[:, None] - idx[None, :]
    causal = rel >= 0
    dmat = jnp.where(causal[None], jnp.exp(jnp.where(causal, rel, 0.0)[None] * lg[:, None, None]), 0.0)
    q_dec = jnp.exp((idx + 1.0)[:, None] * lg[None, :])[None, :, :, None]
    k_dec = jnp.exp((chunk - 1.0 - idx)[:, None] * lg[None, :])[None, :, :, None]
    c_dec = jnp.exp(chunk * lg)[None, :, None, None]

    def to_chunks(a):
        return a.reshape(b, n, chunk, h, a.shape[-1]).transpose(1, 0, 2, 3, 4)

    def step(s, inp):
        qc, kc, vc = inp
        att = jnp.einsum('bihd,bjhd->bhij', qc, kc) * dmat
        inner = jnp.einsum('bhij,bjhe->bihe', att, vc)
        cross = jnp.einsum('bihd,bhde->bihe', qc, s) * q_dec
        s_new = s * c_dec + jnp.einsum('bjhd,bjhe->bhde', kc * k_dec, vc)
        return s_new, inner + cross

    s_fin, out = lax.scan(step, s0, (to_chunks(q), to_chunks(k), to_chunks(v)))
    return out.transpose(1, 0, 2, 3, 4).reshape(b, l, h, v.shape[-1]), s_fin


def retention_branch(qa, ka, va, ga, pos, s0, gn_g, gn_b, chunk):
    b, l, _ = qa.shape
    f32 = jnp.float32
    q = rotary(qa.reshape(b, l, RET_HEADS, RET_DK).astype(f32), pos)
    k = rotary(ka.reshape(b, l, RET_HEADS, RET_DK).astype(f32), pos) * (RET_DK ** -0.5)
    v = va.reshape(b, l, RET_HEADS, RET_DV).astype(f32)
    o, s_new = retention_scan(q, k, v, s0.astype(f32), chunk)
    mu = jnp.mean(o, axis=-1, keepdims=True)
    var = jnp.mean(jnp.square(o - mu), axis=-1, keepdims=True)
    o = ((o - mu) * lax.rsqrt(var + LN_EPS)).reshape(b, l, RET_V_W).astype(qa.dtype)
    return jax.nn.silu(ga) * (o * gn_g + gn_b), s_new.astype(s0.dtype)


def masked_softmax_stats(s, valid):
    s = jnp.where(valid, s, -jnp.inf)
    m = jnp.max(s, axis=-1, keepdims=True)
    p = jnp.exp(s - m)
    l = jnp.sum(p, axis=-1, keepdims=True)
    return p, l, m + jnp.log(l)


def merge_branches(outs, lses):
    w = jax.nn.softmax(jnp.stack(lses, axis=0), axis=0)
    return jnp.sum(jnp.stack(outs, axis=0) * w[..., None], axis=0)


def dilated_prompt(q, k, v):
    b, s, h, dh = q.shape
    span = DIL_BLOCK * max(d for _, d in DIL_PATTERNS)
    s_pad = -(-s // span) * span
    pad = ((0, 0), (0, s_pad - s), (0, 0), (0, 0))
    qp, kp, vp = jnp.pad(q, pad), jnp.pad(k, pad), jnp.pad(v, pad)
    i_loc = np.arange(DIL_BLOCK)[:, None]
    j_loc = np.arange(2 * DIL_BLOCK)[None, :]
    dist = DIL_BLOCK + i_loc - j_loc
    outs, lses = [], []
    for w, d in DIL_PATTERNS:
        m_len = s_pad // d
        nb = m_len // DIL_BLOCK

        def to_sub(a):
            return a.reshape(b, m_len, d, h, dh).transpose(0, 2, 1, 3, 4).reshape(b * d, nb, DIL_BLOCK, h, dh)

        def with_prev(a):
            prev = jnp.pad(a[:, :-1], ((0, 0), (1, 0), (0, 0), (0, 0), (0, 0)))
            return jnp.concatenate([prev, a], axis=2)

        qs = to_sub(qp)
        kb, vb = with_prev(to_sub(kp)), with_prev(to_sub(vp))
        valid = (dist >= 0) & (dist <= w // d)
        valid = valid[None] & ((np.arange(nb)[:, None, None] > 0) | (j_loc >= DIL_BLOCK)[None])
        sc = jnp.einsum('xnihd,xnjhd->xnhij', qs, kb, preferred_element_type=jnp.float32) * DIL_SCALE
        p, l, lse = masked_softmax_stats(sc, jnp.asarray(valid)[None, :, None])
        o = jnp.einsum('xnhij,xnjhd->xnihd', p, vb.astype(jnp.float32)) / jnp.swapaxes(l, 2, 3)
        o = o.reshape(b, d, m_len, h, dh).transpose(0, 2, 1, 3, 4).reshape(b, s_pad, h, dh)[:, :s]
        lse = jnp.swapaxes(lse[..., 0], 2, 3).reshape(b, d, m_len, h).transpose(0, 2, 1, 3).reshape(b, s_pad, h)[:, :s]
        outs.append(o)
        lses.append(lse)
    return merge_branches(outs, lses)


def dilated_sample(q, k, v, ck, cv):
    t = q.shape[1]
    win = ck.shape[1]
    kc = jnp.concatenate([ck.astype(k.dtype), k], axis=1)
    vc = jnp.concatenate([cv.astype(v.dtype), v], axis=1)
    outs, lses = [], []
    for w, d in DIL_PATTERNS:
        idx = win + np.arange(t)[:, None] - d * np.arange(w // d + 1)[None, :]
        valid = idx >= 0
        idx = np.maximum(idx, 0)
        kg, vg = kc[:, idx], vc[:, idx]
        sc = jnp.einsum('bthd,btjhd->bthj', q, kg, preferred_element_type=jnp.float32) * DIL_SCALE
        p, l, lse = masked_softmax_stats(sc, jnp.asarray(valid)[None, :, None, :])
        o = jnp.einsum('bthj,btjhd->bthd', p, vg.astype(jnp.float32)) / l
        outs.append(o)
        lses.append(lse[..., 0])
    return merge_branches(outs, lses)


def split_projection(x, w_in):
    z = jnp.einsum('bld,de->ble', x, w_in)
    sizes = (RET_QK_W, RET_QK_W, RET_V_W, RET_V_W, DIL_W, DIL_W)
    return jnp.split(z, np.cumsum(sizes).tolist(), axis=-1)


def to_heads(a):
    return a.reshape(a.shape[0], a.shape[1], DIL_HEADS, DIL_DH)


def post_mixer(x, ret_o, dil_o, w_out, ln1_g, ln1_b, w_gate, w_up, w_down, ln2_g, ln2_b):
    mix = jnp.einsum('ble,ed->bld', jnp.concatenate([ret_o, dil_o], axis=-1), w_out)
    h = layer_norm(DEEPNORM_ALPHA * x + mix, ln1_g, ln1_b)
    f = jnp.einsum('blf,fd->bld', jax.nn.silu(h @ w_gate) * (h @ w_up), w_down)
    return layer_norm(DEEPNORM_ALPHA * h + f, ln2_g, ln2_b)


def setup_inputs(seed: int = 0) -> dict:
    key = jax.random.key(seed)
    ks = jax.random.split(key, 16)
    f32 = jnp.float32
    win = min(WINDOW_MAX, PAST_LEN)

    def nrm(k, shape, scale):
        return jax.random.normal(k, shape, f32) * scale

    col_scale = np.ones((IN_W,), np.float32)
    va0 = 2 * RET_QK_W
    col_scale[va0:va0 + RET_V_W] = DEEPNORM_BETA
    col_scale[IN_W - DIL_W:] = DEEPNORM_BETA
    return {
        'x_prompt': nrm(ks[0], (BATCH, SEQ, D_MODEL), 1.0),
        'x_sample': nrm(ks[1], (DEC_BATCH, DEC_SEQ, D_MODEL), 1.0),
        'state_ret': nrm(ks[2], (DEPTH, DEC_BATCH, RET_HEADS, RET_DK, RET_DV), 0.5),
        'cache_win_k': nrm(ks[3], (DEPTH, DEC_BATCH, win, DIL_HEADS, DIL_DH), 1.0),
        'cache_win_v': nrm(ks[4], (DEPTH, DEC_BATCH, win, DIL_HEADS, DIL_DH), DEEPNORM_BETA),
        'w_in': nrm(ks[5], (DEPTH, D_MODEL, IN_W), D_MODEL ** -0.5) * jnp.asarray(col_scale),
        'ret_gn_g': 1.0 + nrm(ks[6], (DEPTH, RET_V_W), 0.02),
        'ret_gn_b': nrm(ks[7], (DEPTH, RET_V_W), 0.02),
        'w_out': nrm(ks[8], (DEPTH, MIX_W, D_MODEL), MIX_W ** -0.5 * DEEPNORM_BETA),
        'ln1_g': 1.0 + nrm(ks[9], (DEPTH, D_MODEL), 0.02),
        'ln1_b': nrm(ks[10], (DEPTH, D_MODEL), 0.02),
        'w_gate': nrm(ks[11], (DEPTH, D_MODEL, D_FF), D_MODEL ** -0.5),
        'w_up': nrm(ks[12], (DEPTH, D_MODEL, D_FF), D_MODEL ** -0.5 * DEEPNORM_BETA),
        'w_down': nrm(ks[13], (DEPTH, D_FF, D_MODEL), D_FF ** -0.5 * DEEPNORM_BETA),
        'ln2_g': 1.0 + nrm(ks[14], (DEPTH, D_MODEL), 0.02),
        'ln2_b': nrm(ks[15], (DEPTH, D_MODEL), 0.02),
    }


def reference(x_prompt, x_sample, state_ret, cache_win_k, cache_win_v, w_in, ret_gn_g, ret_gn_b,
              w_out, ln1_g, ln1_b, w_gate, w_up, w_down, ln2_g, ln2_b):
    b, s, _ = x_prompt.shape
    bd, t, _ = x_sample.shape
    win_p = min(WINDOW_MAX, s)
    chunk_p = math.gcd(s, RET_CHUNK)
    pos_p = jnp.arange(s)
    pos_s = PAST_LEN + jnp.arange(t)
    s0_p = jnp.zeros((b, RET_HEADS, RET_DK, RET_DV), jnp.float32)
    xp, xs = x_prompt, x_sample
    rs_p, wk_p, wv_p, rs_s, wk_s, wv_s = [], [], [], [], [], []
    for i in range(DEPTH):
        ffn = (w_out[i], ln1_g[i], ln1_b[i], w_gate[i], w_up[i], w_down[i], ln2_g[i], ln2_b[i])
        qa, ka, va, ga, qb, kb, vb = split_projection(xp, w_in[i])
        ret_o, st = retention_branch(qa, ka, va, ga, pos_p, s0_p, ret_gn_g[i], ret_gn_b[i], chunk_p)
        kh, vh = to_heads(kb), to_heads(vb)
        dil_o = dilated_prompt(to_heads(qb), kh, vh).reshape(b, s, DIL_W).astype(xp.dtype)
        rs_p.append(st)
        wk_p.append(kh[:, s - win_p:])
        wv_p.append(vh[:, s - win_p:])
        xp = post_mixer(xp, ret_o, dil_o, *ffn)
        qa, ka, va, ga, qb, kb, vb = split_projection(xs, w_in[i])
        ret_o, st = retention_branch(qa, ka, va, ga, pos_s, state_ret[i], ret_gn_g[i], ret_gn_b[i], t)
        kh, vh = to_heads(kb), to_heads(vb)
        dil_o = dilated_sample(to_heads(qb), kh, vh, cache_win_k[i], cache_win_v[i]).reshape(bd, t, DIL_W).astype(xs.dtype)
        rs_s.append(st)
        wk_s.append(kh)
        wv_s.append(vh)
        xs = post_mixer(xs, ret_o, dil_o, *ffn)
    return (xp, xs, jnp.stack(rs_p), jnp.stack(wk_p), jnp.stack(wv_p), jnp.stack(rs_s), jnp.stack(wk_s), jnp.stack(wv_s))
```

```python
import functools

import numpy as np
import jax
import jax.numpy as jnp
from jax import lax
from jax.experimental import pallas as pl
from jax.experimental.pallas import tpu as pltpu

F32 = jnp.float32
BF16 = jnp.bfloat16

D_MODEL = 1024
DEPTH = 4
PAST_LEN = 2048
RET_V_W = 512
RET_HEADS = 4
RET_DV = 128
RET_DK = 64
RET_QK_W = 256
ROPE_BASE = 10000.0
DIL_W = 512
DIL_DH = 64
DIL_HEADS = 8
DIL_PATTERNS = ((128, 1), (512, 4), (2048, 16))
WINDOW_MAX = 2048
DIL_BLOCK = 128
DIL_SCALE = DIL_DH ** -0.5
IN_W = 2 * RET_QK_W + 2 * RET_V_W + 3 * DIL_W
D_FF = 2816
DEEPNORM_ALPHA = (2 * DEPTH) ** 0.25
LN_EPS = 1e-5
NEG = -1e30

VMEM_LIMIT = 56 * 1024 * 1024

ROW_TILE = 512
FF_CHUNK = 256
RET_CHUNK = 128


def _params(sem):
    return pltpu.CompilerParams(dimension_semantics=sem, vmem_limit_bytes=VMEM_LIMIT)


def _resident(shape, index_map):
    return pl.BlockSpec(shape, index_map, pipeline_mode=pl.Buffered(1))


def _dot(a, b):
    return jnp.dot(a.astype(BF16), b.astype(BF16), preferred_element_type=F32)


def _dot_nt(a, b):
    return lax.dot_general(a.astype(BF16), b.astype(BF16), (((1,), (1,)), ((), ())),
                           preferred_element_type=F32)


def _dot_tn(a, b):
    return lax.dot_general(a.astype(BF16), b.astype(BF16), (((0,), (0,)), ((), ())),
                           preferred_element_type=F32)


def _silu(x):
    return x * (1.0 / (1.0 + jnp.exp(-x)))


def _layer_norm(x, g, b):
    mu = jnp.mean(x, axis=-1, keepdims=True)
    xc = x - mu
    var = jnp.mean(xc * xc, axis=-1, keepdims=True)
    return xc * lax.rsqrt(var + LN_EPS) * g + b


def _inproj_kernel(x_ref, w_ref, cos_ref, sin_ref,
                   qa_ref, ka_ref, va_ref, ga_ref, qb_ref, kb_ref, vb_ref):
    x = x_ref[...].astype(BF16)

    def proj(lo, hi):
        return jnp.dot(x, w_ref[:, lo:hi], preferred_element_type=F32)

    lane = lax.broadcasted_iota(jnp.int32, (1, 128), 1)
    first_half = (lane % RET_DK) < (RET_DK // 2)

    def rotate(a, c0):
        sw = jnp.where(first_half, pltpu.roll(a, 96, 1), pltpu.roll(a, 32, 1))
        return a * cos_ref[:, c0:c0 + 128] + sw * sin_ref[:, c0:c0 + 128]

    for c in range(RET_QK_W // 128):
        lo = c * 128
        qa_ref[:, lo:lo + 128] = rotate(proj(lo, lo + 128), lo)
        ka_ref[:, lo:lo + 128] = rotate(proj(RET_QK_W + lo, RET_QK_W + lo + 128), lo) * (RET_DK ** -0.5)
    o = 2 * RET_QK_W
    va_ref[...] = proj(o, o + RET_V_W)
    o += RET_V_W
    ga_ref[...] = proj(o, o + RET_V_W)
    o += RET_V_W
    qb_ref[...] = proj(o, o + DIL_W)
    o += DIL_W
    kb_ref[...] = proj(o, o + DIL_W)
    o += DIL_W
    vb_ref[...] = proj(o, o + DIL_W)


def _inproj(x2d, w_in_bf, layer, cos_tab, sin_tab):
    n = x2d.shape[0]
    tm = min(ROW_TILE, n)
    ntab = cos_tab.shape[0] // tm
    row = lambda w: pl.BlockSpec((tm, w), lambda i: (i, 0))
    tab = pl.BlockSpec((tm, RET_QK_W), lambda i: (i % ntab, 0))
    widths = (RET_QK_W, RET_QK_W, RET_V_W, RET_V_W, DIL_W, DIL_W, DIL_W)
    return pl.pallas_call(
        _inproj_kernel,
        grid=(n // tm,),
        in_specs=[row(D_MODEL),
                  _resident((None, D_MODEL, IN_W), lambda i: (layer, 0, 0)),
                  tab, tab],
        out_specs=[row(w) for w in widths],
        out_shape=[jax.ShapeDtypeStruct((n, w), F32) for w in widths],
        compiler_params=_params(("parallel",)),
        name="inproj",
    )(x2d, w_in_bf, cos_tab, sin_tab)


def _rope_tables(pos):
    half = RET_DK // 2
    inv = ROPE_BASE ** (-jnp.linspace(0.0, 1.0, half, dtype=F32))
    ang = pos.astype(F32)[:, None] * inv[None, :]
    cos, sin = jnp.cos(ang), jnp.sin(ang)
    cos_t = jnp.tile(jnp.concatenate([cos, cos], axis=-1), (1, RET_HEADS))
    sin_t = jnp.tile(jnp.concatenate([-sin, sin], axis=-1), (1, RET_HEADS))
    return cos_t, sin_t


def _log_gamma():
    return np.log1p(-np.exp2(-5.0 - np.arange(RET_HEADS, dtype=np.float64)))


def _ret_prompt_tables(c):
    lg = _log_gamma()
    idx = np.arange(c, dtype=np.float64)
    rel = idx[:, None] - idx[None, :]
    dmat = np.where(rel >= 0, np.exp(np.where(rel >= 0, rel, 0.0)[None] * lg[:, None, None]), 0.0)
    dmat = dmat.reshape(RET_HEADS * c, c)
    qdec = np.repeat(np.exp((idx + 1.0)[:, None] * lg[None, :]), RET_DV, axis=1)
    kdec = np.repeat(np.exp((c - 1.0 - idx)[:, None] * lg[None, :]), RET_DK, axis=1)
    cdec = np.repeat(np.exp(c * lg), RET_DV)[None, :]
    hrow = np.arange(RET_QK_W) // RET_DK
    hcol = np.arange(RET_V_W) // RET_DV
    mbd = (hrow[:, None] == hcol[None, :]).astype(np.float64)
    hm = (np.arange(RET_HEADS)[:, None] == hrow[None, :]).astype(np.float64)
    return [jnp.asarray(a, F32) for a in (dmat, qdec, kdec, cdec, mbd, hm)]


def _group_norm_heads(o):
    outs = []
    for h in range(RET_HEADS):
        oh = o[:, h * RET_DV:(h + 1) * RET_DV]
        mu = jnp.mean(oh, axis=-1, keepdims=True)
        oc = oh - mu
        var = jnp.mean(oc * oc, axis=-1, keepdims=True)
        outs.append(oc * lax.rsqrt(var + LN_EPS))
    return jnp.concatenate(outs, axis=-1)


def _ret_prompt_kernel(q_ref, k_ref, v_ref, g_ref, gng_ref, gnb_ref,
                       dmat_ref, qdec_ref, kdec_ref, cdec_ref, mbd_ref, hm_ref,
                       o_ref, st_ref, s_scr):
    j = pl.program_id(1)
    c = q_ref.shape[0]

    @pl.when(j == 0)
    def _():
        s_scr[...] = jnp.zeros_like(s_scr)

    q = q_ref[...]
    k = k_ref[...]
    vb = v_ref[...].astype(BF16)
    qs = jnp.concatenate([q * hm_ref[h:h + 1, :] for h in range(RET_HEADS)], axis=0)
    att = (_dot_nt(qs, k) * dmat_ref[...]).astype(BF16)
    s_old = s_scr[...]
    cross = _dot(q, s_old) * qdec_ref[...]
    upd = _dot_tn(k * kdec_ref[...], vb)
    s_scr[...] = s_old * cdec_ref[...] + upd * mbd_ref[...]
    inner = jnp.concatenate(
        [jnp.dot(att[h * c:(h + 1) * c, :], vb[:, h * RET_DV:(h + 1) * RET_DV],
                 preferred_element_type=F32) for h in range(RET_HEADS)], axis=-1)
    on = _group_norm_heads(inner + cross)
    o_ref[...] = _silu(g_ref[...]) * (on * gng_ref[...] + gnb_ref[...])

    @pl.when(j == pl.num_programs(1) - 1)
    def _():
        for h in range(RET_HEADS):
            st_ref[h] = s_scr[h * RET_DK:(h + 1) * RET_DK, h * RET_DV:(h + 1) * RET_DV]


def _ret_prompt(qa, ka, va, ga, gn_g, gn_b, tables, batch, seq):
    c = RET_CHUNK
    nc = seq // c
    row = lambda w: pl.BlockSpec((c, w), lambda b, j: (b * nc + j, 0))
    full = lambda a: _resident(a.shape, lambda b, j: (0,) * a.ndim)
    return pl.pallas_call(
        _ret_prompt_kernel,
        grid=(batch, nc),
        in_specs=[row(RET_QK_W), row(RET_QK_W), row(RET_V_W), row(RET_V_W),
                  full(gn_g), full(gn_b)] + [full(t) for t in tables],
        out_specs=[row(RET_V_W),
                   pl.BlockSpec((None, RET_HEADS, RET_DK, RET_DV), lambda b, j: (b, 0, 0, 0))],
        out_shape=[jax.ShapeDtypeStruct((batch * seq, RET_V_W), F32),
                   jax.ShapeDtypeStruct((batch, RET_HEADS, RET_DK, RET_DV), F32)],
        scratch_shapes=[pltpu.VMEM((RET_QK_W, RET_V_W), F32)],
        compiler_params=_params(("arbitrary", "arbitrary")),
        name="ret_prompt",
    )(qa, ka, va, ga, gn_g, gn_b, *tables)


def _dil_prompt_kernel(q_ref, kc_ref, kp_ref, vc_ref, vp_ref, o_ref, lse_ref):
    n = pl.program_id(2)
    blk = DIL_BLOCK
    i = lax.broadcasted_iota(jnp.int32, (blk, blk), 0)
    jj = lax.broadcasted_iota(jnp.int32, (blk, blk), 1)
    ok_cur = jj <= i
    ok_prev = (jj >= i) & (n > 0)
    for h in range(DIL_HEADS):
        sl = slice(h * DIL_DH, (h + 1) * DIL_DH)
        qh = q_ref[:, sl].astype(BF16)
        s_cur = jnp.where(ok_cur, _dot_nt(qh, kc_ref[:, sl]) * DIL_SCALE, NEG)
        s_prev = jnp.where(ok_prev, _dot_nt(qh, kp_ref[:, sl]) * DIL_SCALE, NEG)
        m = jnp.maximum(jnp.max(s_cur, axis=-1, keepdims=True),
                        jnp.max(s_prev, axis=-1, keepdims=True))
        p_cur = jnp.exp(s_cur - m)
        p_prev = jnp.exp(s_prev - m)
        l = jnp.sum(p_cur, axis=-1, keepdims=True) + jnp.sum(p_prev, axis=-1, keepdims=True)
        o = (_dot(p_cur, vc_ref[:, sl]) + _dot(p_prev, vp_ref[:, sl])) / l
        o_ref[:, sl] = o
        lse_ref[:, sl] = jnp.broadcast_to(m + jnp.log(l), (blk, DIL_DH))


def _dil_prompt(qb, kb, vb, batch, seq, d):
    m_len = seq // d
    nb = m_len // DIL_BLOCK
    view = lambda a: a.reshape(batch, m_len, d * DIL_W)
    cur = pl.BlockSpec((None, DIL_BLOCK, DIL_W), lambda b, r, n: (b, n, r))
    prev = pl.BlockSpec((None, DIL_BLOCK, DIL_W), lambda b, r, n: (b, jnp.maximum(n - 1, 0), r))
    o, lse = pl.pallas_call(
        _dil_prompt_kernel,
        grid=(batch, d, nb),
        in_specs=[cur, cur, prev, cur, prev],
        out_specs=[cur, cur],
        out_shape=[jax.ShapeDtypeStruct((batch, m_len, d * DIL_W), F32)] * 2,
        compiler_params=_params(("parallel", "parallel", "arbitrary")),
        name="dil_prompt_d%d" % d,
    )(view(qb), view(kb), view(kb), view(vb), view(vb))
    return o.reshape(batch * seq, DIL_W), lse.reshape(batch * seq, DIL_W)


def _post_kernel(n_branch, x_ref, ret_ref, *refs):
    dil_refs = refs[:2 * n_branch] if n_branch > 1 else refs[:1]
    (wout_ref, ln1g_ref, ln1b_ref, wg_ref, wu_ref, wd_ref, ln2g_ref, ln2b_ref,
     y_ref) = refs[len(dil_refs):]
    if n_branch > 1:
        outs = dil_refs[:n_branch]
        lses = [r[...] for r in dil_refs[n_branch:]]
        mx = functools.reduce(jnp.maximum, lses)
        es = [jnp.exp(l - mx) for l in lses]
        den = functools.reduce(lambda a, b: a + b, es)
        dil = functools.reduce(lambda a, b: a + b, [o[...] * (e / den) for o, e in zip(outs, es)])
    else:
        dil = dil_refs[0][...]
    mix = _dot(ret_ref[...], wout_ref[0:RET_V_W, :]) + _dot(dil, wout_ref[RET_V_W:, :])
    h = _layer_norm(DEEPNORM_ALPHA * x_ref[...] + mix, ln1g_ref[...], ln1b_ref[...])
    hb = h.astype(BF16)
    acc = jnp.zeros(h.shape, F32)
    for c in range(D_FF // FF_CHUNK):
        cs = slice(c * FF_CHUNK, (c + 1) * FF_CHUNK)
        gate = jnp.dot(hb, wg_ref[:, cs], preferred_element_type=F32)
        up = jnp.dot(hb, wu_ref[:, cs], preferred_element_type=F32)
        acc = acc + _dot(_silu(gate) * up, wd_ref[cs, :])
    y_ref[...] = _layer_norm(DEEPNORM_ALPHA * h + acc, ln2g_ref[...], ln2b_ref[...])


def _post(x2d, ret_o, dil_list, layer, w_out, ln1_g, ln1_b, w_gate, w_up, w_down, ln2_g, ln2_b):
    n = x2d.shape[0]
    tm = min(ROW_TILE, n)
    n_branch = len(dil_list) // 2 if len(dil_list) > 1 else 1
    row = lambda w: pl.BlockSpec((tm, w), lambda i: (i, 0))
    wsp = lambda a: _resident((None,) + a.shape[1:], lambda i: (layer,) + (0,) * (a.ndim - 1))
    weights = (w_out, ln1_g, ln1_b, w_gate, w_up, w_down, ln2_g, ln2_b)
    return pl.pallas_call(
        functools.partial(_post_kernel, n_branch),
        grid=(n // tm,),
        in_specs=[row(D_MODEL), row(RET_V_W)] + [row(DIL_W)] * len(dil_list) + [wsp(a) for a in weights],
        out_specs=row(D_MODEL),
        out_shape=jax.ShapeDtypeStruct((n, D_MODEL), F32),
        compiler_params=_params(("parallel",)),
        name="post",
    )(x2d, ret_o, *dil_list, *weights)


RS_ROWS = 8
RS_BATCH = 8


def _ret_sample_tables(t_len):
    lg = _log_gamma()
    u = np.arange(t_len * RS_ROWS)
    ut, uh = u // RS_ROWS, u % RS_ROWS
    live = uh < RET_HEADS
    lgu = np.where(live, lg[np.minimum(uh, RET_HEADS - 1)], 0.0)
    rel = ut[:, None] - ut[None, :]
    same = (uh[:, None] == uh[None, :]) & live[:, None] & (rel >= 0)
    dmat = np.where(same, np.exp(np.where(same, rel, 0) * lgu[:, None]), 0.0)
    qdec = np.repeat(np.exp((ut + 1.0) * lgu)[:, None], RET_DV, axis=1)
    kdec = np.repeat(np.exp((t_len - 1.0 - ut) * lgu)[:, None], RET_QK_W, axis=1)
    cdec = np.repeat(np.repeat(np.exp(t_len * lg), RET_DK)[:, None], RET_DV, axis=1)
    hrow = np.arange(RET_QK_W) // RET_DK
    hm = (np.arange(RS_ROWS)[:, None] == hrow[None, :]).astype(np.float64)
    return [jnp.asarray(a, F32) for a in (hm, dmat, qdec, kdec, cdec)]


def _ret_sample_kernel(q_ref, k_ref, v_ref, g_ref, st_ref, gng_ref, gnb_ref,
                       hm_ref, dmat_ref, qdec_ref, kdec_ref, cdec_ref,
                       o_ref, sto_ref, vm_scr, on_scr):
    t_len = q_ref.shape[0] // RS_BATCH
    hm = hm_ref[...]
    vm_scr[...] = jnp.zeros_like(vm_scr)
    for b in range(RS_BATCH):
        r0 = b * t_len

        def spread(ref):
            return jnp.concatenate(
                [jnp.broadcast_to(ref[r0 + t:r0 + t + 1, :], (RS_ROWS, RET_QK_W)) * hm
                 for t in range(t_len)], axis=0)

        qm = spread(q_ref)
        km = spread(k_ref)
        for t in range(t_len):
            for h in range(RET_HEADS):
                u = t * RS_ROWS + h
                vm_scr[u:u + 1, :] = v_ref[r0 + t:r0 + t + 1, h * RET_DV:(h + 1) * RET_DV]
        vm = vm_scr[...]
        s_old = st_ref[b]
        att = _dot_nt(qm, km) * dmat_ref[...]
        o = _dot(att, vm) + _dot(qm, s_old) * qdec_ref[...]
        mu = jnp.mean(o, axis=-1, keepdims=True)
        oc = o - mu
        var = jnp.mean(oc * oc, axis=-1, keepdims=True)
        on = oc * lax.rsqrt(var + LN_EPS)
        for t in range(t_len):
            for h in range(RET_HEADS):
                u = t * RS_ROWS + h
                on_scr[r0 + t:r0 + t + 1, h * RET_DV:(h + 1) * RET_DV] = on[u:u + 1, :]
        sto_ref[b] = s_old * cdec_ref[...] + _dot_tn(km * kdec_ref[...], vm)
    o_ref[...] = _silu(g_ref[...]) * (on_scr[...] * gng_ref[...] + gnb_ref[...])


def _ret_sample(qa, ka, va, ga, state, layer, gn_g, gn_b, tables, t_len):
    n = qa.shape[0]
    bd = n // t_len
    rows = RS_BATCH * t_len
    row = lambda w: pl.BlockSpec((rows, w), lambda i: (i, 0))
    full = lambda a: _resident(a.shape, lambda i: (0,) * a.ndim)
    st_view = state.reshape(DEPTH, bd, RET_QK_W, RET_DV)
    ret_o, st_new = pl.pallas_call(
        _ret_sample_kernel,
        grid=(bd // RS_BATCH,),
        in_specs=[row(RET_QK_W), row(RET_QK_W), row(RET_V_W), row(RET_V_W),
                  pl.BlockSpec((None, RS_BATCH, RET_QK_W, RET_DV), lambda i: (layer, i, 0, 0)),
                  full(gn_g), full(gn_b)] + [full(t) for t in tables],
        out_specs=[row(RET_V_W),
                   pl.BlockSpec((RS_BATCH, RET_QK_W, RET_DV), lambda i: (i, 0, 0))],
        out_shape=[jax.ShapeDtypeStruct((n, RET_V_W), F32),
                   jax.ShapeDtypeStruct((bd, RET_QK_W, RET_DV), F32)],
        scratch_shapes=[pltpu.VMEM((t_len * RS_ROWS, RET_DV), F32),
                        pltpu.VMEM((rows, RET_V_W), F32)],
        compiler_params=_params(("parallel",)),
        name="ret_sample",
    )(qa, ka, va, ga, st_view, gn_g, gn_b, *tables)
    return ret_o, st_new.reshape(bd, RET_HEADS, RET_DK, RET_DV)


def _dil_sample_tables(t_len, win):
    rows = np.arange(win)[None, :]
    t = np.arange(t_len)[:, None]
    cnt = np.zeros((t_len, win))
    cnt_new = np.zeros((t_len, t_len))
    tn = np.arange(t_len)[None, :]
    for w, d in DIL_PATTERNS:
        dist = win + t - rows
        cnt += ((dist % d == 0) & (dist <= w)).astype(np.float64)
        dn = t - tn
        cnt_new += ((dn >= 0) & (dn % d == 0) & (dn <= w)).astype(np.float64)
    return jnp.asarray(cnt, F32), jnp.asarray(cnt_new, F32)


def _dil_sample_kernel(q_ref, kn_ref, vn_ref, ck_ref, cv_ref, cnt_ref, cntn_ref, o_ref):
    t_len = q_ref.shape[0]
    cnt = cnt_ref[...]
    live = cnt > 0.0
    cntn = cntn_ref[...]
    for h in range(DIL_HEADS):
        sl = slice(h * DIL_DH, (h + 1) * DIL_DH)
        qh = q_ref[:, sl]
        knh = kn_ref[:, sl]
        vnh = vn_ref[:, sl]
        s = jnp.where(live, _dot(qh, ck_ref[h]) * DIL_SCALE, NEG)
        qr = qh.astype(BF16).astype(F32)
        s_new = [jnp.sum(qr * knh[t:t + 1, :].astype(BF16).astype(F32), axis=-1, keepdims=True) * DIL_SCALE
                 for t in range(t_len)]
        s_new = [jnp.where(cntn[:, t:t + 1] > 0.0, s_new[t], NEG) for t in range(t_len)]
        m = jnp.max(s, axis=-1, keepdims=True)
        for sn in s_new:
            m = jnp.maximum(m, sn)
        p = cnt * jnp.exp(s - m)
        p_new = [cntn[:, t:t + 1] * jnp.exp(s_new[t] - m) for t in range(t_len)]
        l = jnp.sum(p, axis=-1, keepdims=True)
        o = _dot_nt(p, cv_ref[h])
        for t in range(t_len):
            l = l + p_new[t]
            o = o + p_new[t] * vnh[t:t + 1, :]
        o_ref[:, sl] = o / l


def _dil_sample(qb, kb, vb, cache_kt, cache_vt, layer, tables, t_len):
    n = qb.shape[0]
    bd = n // t_len
    win = cache_kt.shape[-1]
    row = pl.BlockSpec((None, t_len, DIL_W), lambda b: (b, 0, 0))
    cache = pl.BlockSpec((None, None, DIL_HEADS, DIL_DH, win), lambda b: (layer, b, 0, 0, 0))
    full = lambda a: _resident(a.shape, lambda b: (0,) * a.ndim)
    view = lambda a: a.reshape(bd, t_len, DIL_W)
    out = pl.pallas_call(
        _dil_sample_kernel,
        grid=(bd,),
        in_specs=[row, row, row, cache, cache] + [full(t) for t in tables],
        out_specs=row,
        out_shape=jax.ShapeDtypeStruct((bd, t_len, DIL_W), F32),
        compiler_params=_params(("parallel",)),
        name="dil_sample",
    )(view(qb), view(kb), view(vb), cache_kt, cache_vt, *tables)
    return out.reshape(n, DIL_W)


def kernel(x_prompt, x_sample, state_ret, cache_win_k, cache_win_v, w_in, ret_gn_g, ret_gn_b,
           w_out, ln1_g, ln1_b, w_gate, w_up, w_down, ln2_g, ln2_b):
    b, s, _ = x_prompt.shape
    bd, t, _ = x_sample.shape
    win_p = min(WINDOW_MAX, s)
    assert s % (DIL_BLOCK * max(d for _, d in DIL_PATTERNS)) == 0 and s % RET_CHUNK == 0
    assert (b * s) % ROW_TILE == 0 and bd % RS_BATCH == 0

    w_in_bf, w_out_bf = w_in.astype(BF16), w_out.astype(BF16)
    w_gate_bf, w_up_bf, w_down_bf = w_gate.astype(BF16), w_up.astype(BF16), w_down.astype(BF16)
    vec = lambda a: a.reshape(DEPTH, 1, a.shape[-1])
    ln1_g, ln1_b, ln2_g, ln2_b = vec(ln1_g), vec(ln1_b), vec(ln2_g), vec(ln2_b)
    ffn = lambda i: (i, w_out_bf, ln1_g, ln1_b, w_gate_bf, w_up_bf, w_down_bf, ln2_g, ln2_b)

    cos_p, sin_p = _rope_tables(jnp.arange(s))
    cos_s, sin_s = _rope_tables(jnp.tile(PAST_LEN + jnp.arange(t), bd))
    ret_p_tables = _ret_prompt_tables(RET_CHUNK)
    ret_s_tables = _ret_sample_tables(t)
    dil_s_tables = _dil_sample_tables(t, cache_win_k.shape[2])
    cache_kt = jnp.transpose(cache_win_k, (0, 1, 3, 4, 2))
    cache_vt = jnp.transpose(cache_win_v, (0, 1, 3, 4, 2))

    xp = x_prompt.reshape(b * s, D_MODEL)
    xs = x_sample.reshape(bd * t, D_MODEL)
    rs_p, wk_p, wv_p, rs_s, wk_s, wv_s = [], [], [], [], [], []
    for i in range(DEPTH):
        gn_g, gn_b = ret_gn_g[i][None, :], ret_gn_b[i][None, :]
        qa, ka, va, ga, qb, kb, vb = _inproj(xp, w_in_bf, i, cos_p, sin_p)
        ret_o, st = _ret_prompt(qa, ka, va, ga, gn_g, gn_b, ret_p_tables, b, s)
        branches = [_dil_prompt(qb, kb, vb, b, s, d) for _, d in DIL_PATTERNS]
        rs_p.append(st)
        wk_p.append(kb.reshape(b, s, DIL_HEADS, DIL_DH)[:, s - win_p:])
        wv_p.append(vb.reshape(b, s, DIL_HEADS, DIL_DH)[:, s - win_p:])
        xp = _post(xp, ret_o, [o for o, _ in branches] + [l for _, l in branches], *ffn(i))
        qa, ka, va, ga, qb, kb, vb = _inproj(xs, w_in_bf, i, cos_s, sin_s)
        ret_o, st = _ret_sample(qa, ka, va, ga, state_ret, i, gn_g, gn_b, ret_s_tables, t)
        dil_o = _dil_sample(qb, kb, vb, cache_kt, cache_vt, i, dil_s_tables, t)
        rs_s.append(st)
        wk_s.append(kb.reshape(bd, t, DIL_HEADS, DIL_DH))
        wv_s.append(vb.reshape(bd, t, DIL_HEADS, DIL_DH))
        xs = _post(xs, ret_o, [dil_o], *ffn(i))
    return (xp.reshape(b, s, D_MODEL), xs.reshape(bd, t, D_MODEL), jnp.stack(rs_p), jnp.stack(wk_p),
            jnp.stack(wv_p), jnp.stack(rs_s), jnp.stack(wk_s), jnp.stack(wv_s))
```
